```python
import math
import jax, jax.numpy as jnp
from jax import lax
import numpy as np

D_MODEL = 2048
BATCH = 16
SEQ = 2048
DEPTH = 1
DEC_BATCH = 4
DEC_SEQ = 2048
PAST_LEN = 128

RET_HEADS = 8
RET_DK = 128
RET_DV = 128
RET_QK = RET_HEADS * RET_DK
RET_WIDTH = RET_HEADS * RET_DV
CHUNK = 128
DIFF_HEADS = 8
DIFF_DQK = 64
DIFF_DV = 128
DIFF_QK = DIFF_HEADS * 2 * DIFF_DQK
DIFF_WIDTH = DIFF_HEADS * DIFF_DV
Q_BLOCK = 128
MIX_WIDTH = RET_WIDTH + DIFF_WIDTH
P_IN = 2 * RET_QK + 2 * RET_WIDTH + 2 * DIFF_QK + DIFF_WIDTH
PEER_HEADS = 8
PEER_NKEYS = 128
PEER_N = PEER_NKEYS * PEER_NKEYS
PEER_QDIM = 256
PEER_TOPK = 16
TOKEN_BLOCK = 128
EPS = 1e-6

kernel_name = "hybrid_retention_diffattn_peer_encoder"


def _rmsnorm(x, w):
    xf = x.astype(jnp.float32)
    y = xf * lax.rsqrt(jnp.mean(xf * xf, axis=-1, keepdims=True) + EPS)
    return (y * w.astype(jnp.float32)).astype(x.dtype)


def _modulate(h, shift, scale):
    return h * (1 + scale[:, None, :]) + shift[:, None, :]


def _retention_direction(q, k, v, log_gamma, strict):
    b, nh, t, dk = q.shape
    dv = v.shape[-1]
    nc = t // CHUNK
    dt = q.dtype
    qc = q.reshape(b, nh, nc, CHUNK, dk)
    kc = k.reshape(b, nh, nc, CHUNK, dk)
    vc = v.reshape(b, nh, nc, CHUNK, dv)
    lg = log_gamma.astype(jnp.float32)[:, None]
    pos = jnp.arange(CHUNK, dtype=jnp.float32)
    rel = pos[:, None] - pos[None, :]
    mask = (rel > 0) if strict else (rel >= 0)
    intra_decay = jnp.where(mask[None], jnp.exp(lg[:, :, None] * jnp.maximum(rel, 0.0)[None]), 0.0).astype(dt)
    scores = jnp.einsum("bhncd,bhnsd->bhncs", qc, kc) * intra_decay[None, :, None]
    intra = jnp.einsum("bhncs,bhnsv->bhncv", scores, vc)
    k_decay = jnp.exp(lg * (CHUNK - pos)[None]).astype(dt)
    chunk_kv = jnp.einsum("bhnsd,hs,bhnsv->nbhdv", kc, k_decay, vc)
    chunk_decay = jnp.exp(lg[:, 0] * CHUNK).astype(dt)[None, :, None, None]

    def step(state, kv_c):
        return state * chunk_decay + kv_c, state

    _, prev = lax.scan(step, jnp.zeros((b, nh, dk, dv), dt), chunk_kv)
    q_decay = jnp.exp(lg * pos[None]).astype(dt)
    cross = jnp.einsum("bhncd,hc,nbhdv->bhncv", qc, q_decay, prev)
    return (intra + cross).reshape(b, nh, t, dv)


def _hybrid_mixer(h, w_in, w_out, ret_decay_fwd, ret_decay_bwd, ret_gn_w,
                  lam_q1, lam_k1, lam_q2, lam_k2, diff_subln_w, layer_idx):
    b, t, _ = h.shape
    f32 = jnp.float32
    proj = h @ w_in
    sizes = (RET_QK, RET_QK, RET_WIDTH, RET_WIDTH, DIFF_QK, DIFF_QK, DIFF_WIDTH)
    points = [int(p) for p in np.cumsum(sizes)[:-1]]
    rq, rk, rv, rg, dq, dk, dv = jnp.split(proj, points, axis=-1)

    def heads(z, nh, d):
        return z.reshape(b, t, nh, d).transpose(0, 2, 1, 3)

    q = heads(rq, RET_HEADS, RET_DK)
    k = heads(rk, RET_HEADS, RET_DK) * (RET_DK ** -0.5)
    v = heads(rv, RET_HEADS, RET_DV)
    lg_f = jax.nn.log_sigmoid(ret_decay_fwd.astype(f32))
    lg_b = jax.nn.log_sigmoid(ret_decay_bwd.astype(f32))
    o_f = _retention_direction(q, k, v, lg_f, False)
    o_b = jnp.flip(_retention_direction(jnp.flip(q, 2), jnp.flip(k, 2), jnp.flip(v, 2), lg_b, True), 2)
    o = (o_f + o_b).astype(f32)
    mu = jnp.mean(o, axis=-1, keepdims=True)
    var = jnp.mean(jnp.square(o - mu), axis=-1, keepdims=True)
    o = (o - mu) * lax.rsqrt(var + EPS)
    o = o.transpose(0, 2, 1, 3).reshape(b, t, RET_WIDTH) * ret_gn_w.astype(f32)
    ret_out = (jax.nn.silu(rg.astype(f32)) * o).astype(h.dtype)

    lam_init = 0.8 - 0.6 * math.exp(-0.3 * layer_idx)
    lam = (jnp.exp(jnp.sum(lam_q1.astype(f32) * lam_k1.astype(f32)))
           - jnp.exp(jnp.sum(lam_q2.astype(f32) * lam_k2.astype(f32))) + lam_init)
    qd = dq.reshape(b, t, DIFF_HEADS, 2, DIFF_DQK).transpose(0, 2, 3, 1, 4)
    kd = dk.reshape(b, t, DIFF_HEADS, 2, DIFF_DQK).transpose(0, 2, 3, 1, 4)
    vd = heads(dv, DIFF_HEADS, DIFF_DV)
    nq = t // Q_BLOCK
    qb = jnp.moveaxis(qd.reshape(b, DIFF_HEADS, 2, nq, Q_BLOCK, DIFF_DQK), 3, 0)
    slopes = 2.0 ** (-8.0 * jnp.arange(1, DIFF_HEADS + 1, dtype=f32) / DIFF_HEADS)
    kpos = jnp.arange(t, dtype=f32)

    def attend(args):
        qi, start = args
        s = jnp.einsum("bhmqd,bhmkd->bhmqk", qi, kd).astype(f32) * (DIFF_DQK ** -0.5)
        qpos = start.astype(f32) + jnp.arange(Q_BLOCK, dtype=f32)
        dist = jnp.abs(qpos[:, None] - kpos[None, :])
        s = s - slopes[None, :, None, None, None] * dist[None, None, None]
        p = jax.nn.softmax(s, axis=-1)
        a = p[:, :, 0] - lam * p[:, :, 1]
        return jnp.einsum("bhqk,bhkv->bhqv", a.astype(vd.dtype), vd)

    od = lax.map(attend, (qb, jnp.arange(nq, dtype=jnp.int32) * Q_BLOCK))
    od = jnp.moveaxis(od, 0, 2).reshape(b, DIFF_HEADS, t, DIFF_DV).astype(f32)
    od = od * lax.rsqrt(jnp.mean(od * od, axis=-1, keepdims=True) + EPS) * diff_subln_w.astype(f32) * (1.0 - lam_init)
    diff_out = od.transpose(0, 2, 1, 3).reshape(b, t, DIFF_WIDTH).astype(h.dtype)

    return jnp.concatenate([ret_out, diff_out], axis=-1) @ w_out


def _peer(h, w_query, sub_keys, u_tab, v_tab):
    b, t, d = h.shape
    xf = h.reshape(b * t // TOKEN_BLOCK, TOKEN_BLOCK, d)

    def block(xb):
        q = (xb @ w_query).reshape(TOKEN_BLOCK, PEER_HEADS, 2, PEER_QDIM // 2)
        s = jnp.einsum("nhpd,hpkd->nhpk", q, sub_keys).astype(jnp.float32)
        sv, si = lax.top_k(s, PEER_TOPK)
        cand = sv[:, :, 0, :, None] + sv[:, :, 1, None, :]
        cidx = si[:, :, 0, :, None] * PEER_NKEYS + si[:, :, 1, None, :]
        cand = cand.reshape(TOKEN_BLOCK, PEER_HEADS, PEER_TOPK * PEER_TOPK)
        cidx = cidx.reshape(TOKEN_BLOCK, PEER_HEADS, PEER_TOPK * PEER_TOPK)
        top_s, top_pos = lax.top_k(cand, PEER_TOPK)
        eidx = jnp.take_along_axis(cidx, top_pos, axis=-1).reshape(TOKEN_BLOCK, PEER_HEADS * PEER_TOPK)
        g = jax.nn.softmax(top_s, axis=-1).reshape(TOKEN_BLOCK, PEER_HEADS * PEER_TOPK)
        u = u_tab[eidx]
        hid = jax.nn.gelu(jnp.einsum("nd,ned->ne", xb, u), approximate=False)
        w = (g * hid.astype(jnp.float32)).astype(xb.dtype)
        return jnp.einsum("ne,ned->nd", w, v_tab[eidx])

    return lax.map(block, xf).reshape(b, t, d)


def _layer(x, c, l, ada_w, ada_b, mix_pre_w, mix_post_w, ffn_pre_w, ffn_post_w,
           w_in, w_out, ret_decay_fwd, ret_decay_bwd, ret_gn_w,
           diff_lambda_q1, diff_lambda_k1, diff_lambda_q2, diff_lambda_k2, diff_subln_w,
           peer_w_query, peer_sub_keys, peer_u, peer_v):
    mod = jax.nn.silu(c) @ ada_w[l] + ada_b[l]
    sh1, sc1, g1, sh2, sc2, g2 = jnp.split(mod, 6, axis=-1)
    h = _modulate(_rmsnorm(x, mix_pre_w[l]), sh1, sc1)
    o = _hybrid_mixer(h, w_in[l], w_out[l], ret_decay_fwd[l], ret_decay_bwd[l], ret_gn_w[l],
                      diff_lambda_q1[l], diff_lambda_k1[l], diff_lambda_q2[l], diff_lambda_k2[l],
                      diff_subln_w[l], l)
    x = x + g1[:, None, :] * _rmsnorm(o, mix_post_w[l])
    h = _modulate(_rmsnorm(x, ffn_pre_w[l]), sh2, sc2)
    o = _peer(h, peer_w_query[l], peer_sub_keys[l], peer_u[l], peer_v[l])
    return x + g2[:, None, :] * _rmsnorm(o, ffn_post_w[l])


def setup_inputs(seed: int = 0) -> dict:
    key = jax.random.key(seed)
    ks = jax.random.split(key, 32)
    f32 = jnp.float32
    nrm = lambda k, shape, s: jax.random.normal(k, shape, f32) * s
    base_logit = jnp.log(2.0 ** (5.0 + jnp.arange(RET_HEADS, dtype=f32)) - 1.0)
    return {
        "x_prompt": nrm(ks[0], (BATCH, SEQ, D_MODEL), 1.0),
        "x_sample": nrm(ks[1], (DEC_BATCH, DEC_SEQ, D_MODEL), 1.0),
        "c_prompt": nrm(ks[2], (BATCH, D_MODEL), 1.0),
        "c_sample": nrm(ks[3], (DEC_BATCH, D_MODEL), 1.0),
        "ada_w": nrm(ks[4], (DEPTH, D_MODEL, 6 * D_MODEL), 0.5 * D_MODEL ** -0.5),
        "ada_b": nrm(ks[5], (DEPTH, 6 * D_MODEL), 0.02),
        "mix_pre_w": 1.0 + nrm(ks[6], (DEPTH, D_MODEL), 0.02),
        "mix_post_w": 1.0 + nrm(ks[7], (DEPTH, D_MODEL), 0.02),
        "ffn_pre_w": 1.0 + nrm(ks[8], (DEPTH, D_MODEL), 0.02),
        "ffn_post_w": 1.0 + nrm(ks[9], (DEPTH, D_MODEL), 0.02),
        "w_in": nrm(ks[10], (DEPTH, D_MODEL, P_IN), D_MODEL ** -0.5),
        "w_out": nrm(ks[11], (DEPTH, MIX_WIDTH, D_MODEL), MIX_WIDTH ** -0.5),
        "ret_decay_fwd": base_logit[None] + nrm(ks[12], (DEPTH, RET_HEADS), 0.1),
        "ret_decay_bwd": base_logit[None] + nrm(ks[13], (DEPTH, RET_HEADS), 0.1),
        "ret_gn_w": 1.0 + nrm(ks[14], (DEPTH, RET_WIDTH), 0.02),
        "diff_lambda_q1": nrm(ks[15], (DEPTH, DIFF_DQK), 0.1),
        "diff_lambda_k1": nrm(ks[16], (DEPTH, DIFF_DQK), 0.1),
        "diff_lambda_q2": nrm(ks[17], (DEPTH, DIFF_DQK), 0.1),
        "diff_lambda_k2": nrm(ks[18], (DEPTH, DIFF_DQK), 0.1),
        "diff_subln_w": 1.0 + nrm(ks[19], (DEPTH, DIFF_DV), 0.02),
        "peer_w_query": nrm(ks[20], (DEPTH, D_MODEL, PEER_HEADS * PEER_QDIM), D_MODEL ** -0.5),
        "peer_sub_keys": nrm(ks[21], (DEPTH, PEER_HEADS, 2, PEER_NKEYS, PEER_QDIM // 2), (PEER_QDIM // 2) ** -0.5),
        "peer_u": nrm(ks[22], (DEPTH, PEER_N, D_MODEL), D_MODEL ** -0.5),
        "peer_v": nrm(ks[23], (DEPTH, PEER_N, D_MODEL), D_MODEL ** -0.5),
    }


def reference(x_prompt, x_sample, c_prompt, c_sample, ada_w, ada_b, mix_pre_w, mix_post_w,
              ffn_pre_w, ffn_post_w, w_in, w_out, ret_decay_fwd, ret_decay_bwd, ret_gn_w,
              diff_lambda_q1, diff_lambda_k1, diff_lambda_q2, diff_lambda_k2, diff_subln_w,
              peer_w_query, peer_sub_keys, peer_u, peer_v):
    def run(x, c):
        for l in range(DEPTH):
            x = _layer(x, c, l, ada_w, ada_b, mix_pre_w, mix_post_w, ffn_pre_w, ffn_post_w,
                       w_in, w_out, ret_decay_fwd, ret_decay_bwd, ret_gn_w,
                       diff_lambda_q1, diff_lambda_k1, diff_lambda_q2, diff_lambda_k2, diff_subln_w,
                       peer_w_query, peer_sub_keys, peer_u, peer_v)
        return x

    y_prompt = run(x_prompt, c_prompt)
    y_sample = run(x_sample, c_sample)
    return (y_prompt, y_sample)
```

```python
import functools
import math

import jax
import jax.numpy as jnp
from jax import lax
from jax.experimental import pallas as pl
from jax.experimental.pallas import tpu as pltpu

F32 = jnp.float32
BF16 = jnp.bfloat16
EPS = 1e-6

RET_HEADS = 8
RET_D = 128
DIFF_HEADS = 8
DIFF_DQK = 64
DIFF_DV = 128
PEER_HEADS = 8
PEER_NKEYS = 128
PEER_HALF = 128
PEER_TOPK = 16
RET_CHUNK = 256

VMEM_LIMIT = 56 * 1024 * 1024


def _params(sem):
    return pltpu.CompilerParams(dimension_semantics=sem, vmem_limit_bytes=VMEM_LIMIT)


def _rms(x, w):
    return x * lax.rsqrt(jnp.mean(x * x, axis=-1, keepdims=True) + EPS) * w


def _adaln_kernel(c_ref, w_ref, b_ref, o_ref):
    c = c_ref[...]
    a = c / (1.0 + jnp.exp(-c))
    o_ref[...] = jnp.dot(a.astype(BF16), w_ref[...].astype(BF16),
                         preferred_element_type=F32) + b_ref[...]


def _adaln(c, w, b):
    nb, d = c.shape
    n_out = w.shape[1]
    tn = 1024
    return pl.pallas_call(
        _adaln_kernel,
        grid=(n_out // tn,),
        in_specs=[pl.BlockSpec((nb, d), lambda j: (0, 0)),
                  pl.BlockSpec((d, tn), lambda j: (0, j)),
                  pl.BlockSpec((1, tn), lambda j: (0, j))],
        out_specs=pl.BlockSpec((nb, tn), lambda j: (0, j)),
        out_shape=jax.ShapeDtypeStruct((nb, n_out), F32),
        compiler_params=_params(("arbitrary",)),
        name="adaln",
    )(c, w, b.reshape(1, n_out))


def _proj_kernel(x_ref, mod_ref, nw_ref, w_ref, o_ref, h_ref):
    @pl.when(pl.program_id(1) == 0)
    def _():
        h = _rms(x_ref[...], nw_ref[...]) * (1.0 + mod_ref[0, 1:2, :]) + mod_ref[0, 0:1, :]
        h_ref[...] = h.astype(BF16)

    o_ref[...] = jnp.dot(h_ref[...], w_ref[...], preferred_element_type=F32).astype(BF16)


def _in_proj(x2d, mod, norm_w, w_bf, seq):
    n, d = x2d.shape
    p_in = w_bf.shape[1]
    tm = min(1024, seq)
    tn = 1024
    per_batch = seq // tm
    return pl.pallas_call(
        _proj_kernel,
        grid=(n // tm, p_in // tn),
        in_specs=[pl.BlockSpec((tm, d), lambda i, j: (i, 0)),
                  pl.BlockSpec((1, 6, d), lambda i, j: (i // per_batch, 0, 0)),
                  pl.BlockSpec((1, d), lambda i, j: (0, 0)),
                  pl.BlockSpec((d, tn), lambda i, j: (0, j))],
        out_specs=pl.BlockSpec((tm, tn), lambda i, j: (i, j)),
        out_shape=jax.ShapeDtypeStruct((n, p_in), BF16),
        scratch_shapes=[pltpu.VMEM((tm, d), BF16)],
        compiler_params=_params(("arbitrary", "arbitrary")),
        name="in_proj",
    )(x2d, mod, norm_w, w_bf)


def _log_sigmoid(x):
    return jnp.minimum(x, 0.0) - jnp.log1p(jnp.exp(-jnp.abs(x)))


def _retention_kernel(q_ref, k_ref, v_ref, g_ref, df_ref, db_ref, gn_ref, o_ref, kv_ref, s_ref,
                      *, seq):
    c = RET_CHUNK
    nc = seq // c
    d = RET_D
    lgf = _log_sigmoid(df_ref[0])
    lgb = _log_sigmoid(db_ref[0])
    scale = d ** -0.5

    ri = lax.broadcasted_iota(jnp.int32, (c, c), 0)
    ci = lax.broadcasted_iota(jnp.int32, (c, c), 1)
    rel = (ri - ci).astype(F32)
    dmat = jnp.where(rel >= 0.0,
                     jnp.exp(lgf * jnp.maximum(rel, 0.0)),
                     jnp.exp(lgb * jnp.maximum(-rel, 0.0))) * scale
    pos = lax.broadcasted_iota(jnp.int32, (c, d), 0).astype(F32)
    lgf_d = lgf[:, :d]
    lgb_d = lgb[:, :d]
    q_dec_f = jnp.exp(lgf_d * pos)
    q_dec_b = jnp.exp(lgb_d * (c - 1.0 - pos))
    k_dec_f = jnp.exp(lgf_d * (c - pos)) * scale
    k_dec_b = jnp.exp(lgb_d * (pos + 1.0)) * scale
    chunk_f = jnp.exp(lgf_d * float(c))
    chunk_b = jnp.exp(lgb_d * float(c))

    for n in range(nc):
        kc = k_ref[0, n * c:(n + 1) * c, :].astype(F32)
        vc = v_ref[0, n * c:(n + 1) * c, :]
        kk = jnp.concatenate([kc * k_dec_f, kc * k_dec_b], axis=1)
        kv_ref[n] = jnp.dot(kk.T.astype(BF16), vc, preferred_element_type=F32)

    state = jnp.zeros((d, d), F32)
    for n in range(nc):
        s_ref[n, 0:d, :] = state.astype(BF16)
        state = state * chunk_f + kv_ref[n, 0:d, :]
    state = jnp.zeros((d, d), F32)
    for n in range(nc - 1, -1, -1):
        s_ref[n, d:2 * d, :] = state.astype(BF16)
        state = state * chunk_b + kv_ref[n, d:2 * d, :]

    gn_w = gn_ref[...]
    for n in range(nc):
        sl = slice(n * c, (n + 1) * c)
        qc = q_ref[0, sl, :]
        kc = k_ref[0, sl, :]
        vc = v_ref[0, sl, :]
        sc = lax.dot_general(qc, kc, (((1,), (1,)), ((), ())), preferred_element_type=F32)
        intra = jnp.dot((sc * dmat).astype(BF16), vc, preferred_element_type=F32)
        qf = qc.astype(F32)
        qq = jnp.concatenate([qf * q_dec_f, qf * q_dec_b], axis=1).astype(BF16)
        cross = jnp.dot(qq, s_ref[n], preferred_element_type=F32)
        o = intra + cross
        mu = jnp.mean(o, axis=-1, keepdims=True)
        var = jnp.mean(jnp.square(o - mu), axis=-1, keepdims=True)
        o = (o - mu) * lax.rsqrt(var + EPS) * gn_w
        g = g_ref[0, sl, :].astype(F32)
        o_ref[0, sl, :] = ((g / (1.0 + jnp.exp(-g))) * o).astype(BF16)


def _retention(p3, decay_f, decay_b, gn_w):
    nb, seq, _ = p3.shape
    h, d, c = RET_HEADS, RET_D, RET_CHUNK
    df = jnp.broadcast_to(decay_f.astype(F32)[:, None, None], (h, 1, c))
    db = jnp.broadcast_to(decay_b.astype(F32)[:, None, None], (h, 1, c))
    blk = lambda off: pl.BlockSpec((1, seq, d), lambda b, hh: (b, 0, off + hh))
    return pl.pallas_call(
        functools.partial(_retention_kernel, seq=seq),
        grid=(nb, h),
        in_specs=[blk(0), blk(h), blk(2 * h), blk(3 * h),
                  pl.BlockSpec((1, 1, c), lambda b, hh: (hh, 0, 0)),
                  pl.BlockSpec((1, 1, c), lambda b, hh: (hh, 0, 0)),
                  pl.BlockSpec((1, d), lambda b, hh: (0, hh))],
        out_specs=pl.BlockSpec((1, seq, d), lambda b, hh: (b, 0, hh)),
        out_shape=jax.ShapeDtypeStruct((nb, seq, h * d), BF16),
        scratch_shapes=[pltpu.VMEM((seq // c, 2 * d, d), F32),
                        pltpu.VMEM((seq // c, 2 * d, d), BF16)],
        compiler_params=_params(("arbitrary", "arbitrary")),
        name="retention",
    )(p3, p3, p3, p3, df, db, gn_w.reshape(1, h * d))


def _diff_kernel(q_ref, k_ref, v_ref, slope_ref, lq1_ref, lk1_ref, lq2_ref, lk2_ref, sw_ref,
                 o_ref, bias_ref, *, tq, seq, lam_init):
    qi = pl.program_id(1)

    @pl.when(pl.program_id(2) == 0)
    def _():
        qpos = (lax.broadcasted_iota(jnp.int32, (tq, seq), 0) + qi * tq).astype(F32)
        kpos = lax.broadcasted_iota(jnp.int32, (tq, seq), 1).astype(F32)
        bias_ref[...] = slope_ref[0] * jnp.abs(qpos - kpos)

    lam = (jnp.exp(jnp.sum(lq1_ref[...] * lk1_ref[...], axis=-1, keepdims=True))
           - jnp.exp(jnp.sum(lq2_ref[...] * lk2_ref[...], axis=-1, keepdims=True)) + lam_init)

    q = q_ref[0]
    lane = lax.broadcasted_iota(jnp.int32, q.shape, 1)
    zero = jnp.zeros_like(q)
    k = k_ref[0]
    nt = (((1,), (1,)), ((), ()))
    bias = bias_ref[...]
    scale = DIFF_DQK ** -0.5

    def probs(qm):
        s = lax.dot_general(qm, k, nt, preferred_element_type=F32) * scale - bias
        e = jnp.exp(s - jnp.max(s, axis=-1, keepdims=True))
        return e, jnp.sum(e, axis=-1, keepdims=True)

    e1, l1 = probs(jnp.where(lane < DIFF_DQK, q, zero))
    e2, l2 = probs(jnp.where(lane >= DIFF_DQK, q, zero))
    a = e1 * (1.0 / l1) - e2 * (lam / l2)
    od = jnp.dot(a.astype(BF16), v_ref[0], preferred_element_type=F32)
    od = od * lax.rsqrt(jnp.mean(od * od, axis=-1, keepdims=True) + EPS) * sw_ref[...] * (1.0 - lam_init)
    o_ref[0] = od.astype(BF16)


def _diff_attention(p3, lq1, lk1, lq2, lk2, subln_w, layer_idx):
    nb, seq, _ = p3.shape
    h = DIFF_HEADS
    d = DIFF_DV
    tq = min(256, seq)
    lam_init = 0.8 - 0.6 * math.exp(-0.3 * layer_idx)
    slopes = 2.0 ** (-8.0 * jnp.arange(1, h + 1, dtype=F32) / h)
    slopes = jnp.broadcast_to(slopes[:, None, None], (h, 1, seq))
    off_q = 4 * RET_HEADS
    off_k = off_q + h
    off_v = off_k + h
    vec = lambda: pl.BlockSpec((1, DIFF_DQK), lambda hh, qi, b: (0, 0))
    return pl.pallas_call(
        functools.partial(_diff_kernel, tq=tq, seq=seq, lam_init=lam_init),
        grid=(h, seq // tq, nb),
        in_specs=[pl.BlockSpec((1, tq, d), lambda hh, qi, b: (b, qi, off_q + hh)),
                  pl.BlockSpec((1, seq, d), lambda hh, qi, b: (b, 0, off_k + hh)),
                  pl.BlockSpec((1, seq, d), lambda hh, qi, b: (b, 0, off_v + hh)),
                  pl.BlockSpec((1, 1, seq), lambda hh, qi, b: (hh, 0, 0)),
                  vec(), vec(), vec(), vec(),
                  pl.BlockSpec((1, d), lambda hh, qi, b: (0, 0))],
        out_specs=pl.BlockSpec((1, tq, d), lambda hh, qi, b: (b, qi, hh)),
        out_shape=jax.ShapeDtypeStruct((nb, seq, h * d), BF16),
        scratch_shapes=[pltpu.VMEM((tq, seq), F32)],
        compiler_params=_params(("arbitrary", "arbitrary", "arbitrary")),
        name="diff_attention",
    )(p3, p3, p3, slopes, lq1.reshape(1, -1), lk1.reshape(1, -1), lq2.reshape(1, -1),
      lk2.reshape(1, -1), subln_w.reshape(1, d))


def _out_kernel(r_ref, a_ref, x_ref, mod_ref, w_ref, postw_ref, prew_ref, x1_ref, h2_ref):
    half = r_ref.shape[1]
    o = (jnp.dot(r_ref[...], w_ref[0:half, :], preferred_element_type=F32)
         + jnp.dot(a_ref[...], w_ref[half:2 * half, :], preferred_element_type=F32))
    x1 = x_ref[...] + mod_ref[0, 2:3, :] * _rms(o, postw_ref[...])
    x1_ref[...] = x1
    h2 = _rms(x1, prew_ref[...]) * (1.0 + mod_ref[0, 4:5, :]) + mod_ref[0, 3:4, :]
    h2_ref[...] = h2.astype(BF16)


def _out_proj(ret, att, x2d, mod, w_bf, post_w, pre_w, seq):
    n, d = x2d.shape
    half = ret.shape[1]
    tm = min(512, seq)
    per_batch = seq // tm
    row = lambda: pl.BlockSpec((1, d), lambda i: (0, 0))
    return pl.pallas_call(
        _out_kernel,
        grid=(n // tm,),
        in_specs=[pl.BlockSpec((tm, half), lambda i: (i, 0)),
                  pl.BlockSpec((tm, half), lambda i: (i, 0)),
                  pl.BlockSpec((tm, d), lambda i: (i, 0)),
                  pl.BlockSpec((1, 6, d), lambda i: (i // per_batch, 0, 0)),
                  pl.BlockSpec((2 * half, d), lambda i: (0, 0)),
                  row(), row()],
        out_specs=[pl.BlockSpec((tm, d), lambda i: (i, 0)),
                   pl.BlockSpec((tm, d), lambda i: (i, 0))],
        out_shape=[jax.ShapeDtypeStruct((n, d), F32), jax.ShapeDtypeStruct((n, d), BF16)],
        compiler_params=_params(("arbitrary",)),
        name="out_proj",
    )(ret, att, x2d, mod, w_bf, post_w, pre_w)


def _hyperbola_cells():
    k = PEER_TOPK
    return [(a, b) for a in range(k) for b in range(k) if (a + 1) * (b + 1) <= k]


def _route_kernel(h_ref, wq_ref, sk_ref, a_ref, b_ref, c_ref, r_ref,
                  q_s, e1_s, r1_s, sv1_s, sv2_s, cnt_s, iz_s):
    nk, topk, nh = PEER_NKEYS, PEER_TOPK, PEER_HEADS
    tk = h_ref.shape[0]
    nt = (((1,), (1,)), ((), ()))
    q_s[...] = lax.dot_general(wq_ref[...], h_ref[...], nt, preferred_element_type=F32).astype(BF16)
    key_idx = lax.broadcasted_iota(jnp.int32, (nk, tk), 0).astype(F32)

    def top16(s):
        work = s
        rank = jnp.full((nk, tk), float(topk), F32)
        vals = []
        for r in range(topk):
            m = jnp.max(work, axis=0, keepdims=True)
            first = jnp.min(jnp.where(work == m, key_idx, float(nk)), axis=0, keepdims=True)
            sel = key_idx == first
            rank = jnp.where(sel, float(r), rank)
            work = jnp.where(sel, -jnp.inf, work)
            vals.append(m)
        return rank, vals

    def per_head(h, carry):
        for p in range(2):
            hp = 2 * h + p
            qh = q_s[pl.ds(pl.multiple_of(hp * PEER_HALF, PEER_HALF), PEER_HALF), :]
            s = jnp.dot(sk_ref[hp], qh, preferred_element_type=F32)
            rank, vals = top16(s)
            e = jnp.exp(s - vals[0])
            sv = sv1_s if p == 0 else sv2_s
            for r in range(topk):
                sv[r, pl.ds(h, 1), :] = vals[r]
            if p == 0:
                e1_s[h] = e
                r1_s[h] = rank
            else:
                b_ref[h] = e
                r_ref[h] = rank
        return carry

    lax.fori_loop(0, nh, per_head, 0)

    cells = _hyperbola_cells()
    vals = [sv1_s[a] + sv2_s[b] for a, b in cells]
    beaten = [jnp.zeros((nh, tk), F32) for _ in cells]
    for x in range(len(cells)):
        for y in range(x + 1, len(cells)):
            x_wins = (vals[x] >= vals[y]).astype(F32)
            beaten[y] = beaten[y] + x_wins
            beaten[x] = beaten[x] + (1.0 - x_wins)
    z = jnp.zeros((nh, tk), F32)
    count = [jnp.zeros((nh, tk), F32) for _ in range(topk)]
    for i, (a, b) in enumerate(cells):
        sel = (beaten[i] < float(topk)).astype(F32)
        count[a] = count[a] + sel
        z = z + sel * jnp.exp(vals[i] - vals[0])
    for a in range(topk):
        cnt_s[a] = count[a]
    iz_s[...] = 1.0 / z

    def finish(h, carry):
        rank1 = r1_s[h]
        cvec = jnp.zeros((nk, tk), F32)
        for a in range(topk):
            cvec = jnp.where(rank1 == float(a), cnt_s[a, pl.ds(h, 1), :], cvec)
        c_ref[h] = cvec
        a_ref[h] = e1_s[h] * iz_s[pl.ds(h, 1), :]
        return carry

    lax.fori_loop(0, nh, finish, 0)


def _peer_route(h2, wq_t, sub_keys):
    n, d = h2.shape
    nh, nk, topk = PEER_HEADS, PEER_NKEYS, PEER_TOPK
    tk = 256
    out = jax.ShapeDtypeStruct((nh, nk, n), F32)
    ospec = lambda: pl.BlockSpec((nh, nk, tk), lambda i: (0, 0, i))
    return pl.pallas_call(
        _route_kernel,
        grid=(n // tk,),
        in_specs=[pl.BlockSpec((tk, d), lambda i: (i, 0)),
                  pl.BlockSpec(wq_t.shape, lambda i: (0, 0)),
                  pl.BlockSpec(sub_keys.shape, lambda i: (0, 0, 0))],
        out_specs=[ospec(), ospec(), ospec(), ospec()],
        out_shape=[out, out, out, out],
        scratch_shapes=[pltpu.VMEM((wq_t.shape[0], tk), BF16),
                        pltpu.VMEM((nh, nk, tk), F32),
                        pltpu.VMEM((nh, nk, tk), F32),
                        pltpu.VMEM((topk, nh, tk), F32),
                        pltpu.VMEM((topk, nh, tk), F32),
                        pltpu.VMEM((topk, nh, tk), F32),
                        pltpu.VMEM((nh, tk), F32)],
        compiler_params=_params(("arbitrary",)),
        name="peer_route",
    )(h2, wq_t, sub_keys)


def _expert_kernel(h_ref, u_ref, vt_ref, a_ref, b_ref, c_ref, r_ref, o_ref, w_s, *, rows):
    j = pl.program_id(1)
    nk = PEER_NKEYS
    nt = (((1,), (1,)), ((), ()))
    hid = lax.dot_general(u_ref[...], h_ref[...], nt, preferred_element_type=F32)
    for ii in range(rows):
        i = j * rows + ii
        x = hid[ii * nk:(ii + 1) * nk, :]
        gate = jnp.zeros_like(x)
        for h in range(PEER_HEADS):
            a = a_ref[h, pl.ds(i, 1), :]
            c = c_ref[h, pl.ds(i, 1), :]
            gate = gate + jnp.where(r_ref[h] < c, b_ref[h], 0.0) * a
        act = 0.5 * x * (1.0 + lax.erf(x * (2.0 ** -0.5)))
        w_s[ii * nk:(ii + 1) * nk, :] = (gate * act).astype(BF16)
    part = jnp.dot(vt_ref[...], w_s[...], preferred_element_type=F32)

    @pl.when(j == 0)
    def _():
        o_ref[...] = part

    @pl.when(j > 0)
    def _():
        o_ref[...] += part


def _peer_experts(h2, u_bf, vt_bf, a, b, c, r):
    n, d = h2.shape
    n_exp = u_bf.shape[0]
    nh, nk = PEER_HEADS, PEER_NKEYS
    tn = 512
    rows = 8
    eb = rows * nk
    gspec = lambda: pl.BlockSpec((nh, nk, tn), lambda i, j: (0, 0, i))
    return pl.pallas_call(
        functools.partial(_expert_kernel, rows=rows),
        grid=(n // tn, n_exp // eb),
        in_specs=[pl.BlockSpec((tn, d), lambda i, j: (i, 0)),
                  pl.BlockSpec((eb, d), lambda i, j: (j, 0)),
                  pl.BlockSpec((d, eb), lambda i, j: (0, j)),
                  gspec(), gspec(), gspec(), gspec()],
        out_specs=pl.BlockSpec((d, tn), lambda i, j: (0, i)),
        out_shape=jax.ShapeDtypeStruct((d, n), F32),
        scratch_shapes=[pltpu.VMEM((eb, tn), BF16)],
        compiler_params=_params(("arbitrary", "arbitrary")),
        name="peer_experts",
    )(h2, u_bf, vt_bf, a, b, c, r)


def _final_kernel(ot_ref, x1_ref, mod_ref, w_ref, y_ref):
    o = ot_ref[...].T
    y_ref[...] = x1_ref[...] + mod_ref[0, 5:6, :] * _rms(o, w_ref[...])


def _final(o_t, x1, mod, post_w, seq):
    n, d = x1.shape
    tm = min(512, seq)
    per_batch = seq // tm
    return pl.pallas_call(
        _final_kernel,
        grid=(n // tm,),
        in_specs=[pl.BlockSpec((d, tm), lambda i: (0, i)),
                  pl.BlockSpec((tm, d), lambda i: (i, 0)),
                  pl.BlockSpec((1, 6, d), lambda i: (i // per_batch, 0, 0)),
                  pl.BlockSpec((1, d), lambda i: (0, 0))],
        out_specs=pl.BlockSpec((tm, d), lambda i: (i, 0)),
        out_shape=jax.ShapeDtypeStruct((n, d), F32),
        compiler_params=_params(("arbitrary",)),
        name="final_residual",
    )(o_t, x1, mod, post_w)


def _layer(x, mod, l, w, layer_idx):
    nb, seq, d = x.shape
    x2d = x.reshape(nb * seq, d)
    p = _in_proj(x2d, mod, w["mix_pre_w"], w["w_in"], seq)
    p3 = p.reshape(nb, seq, -1)
    ret = _retention(p3, w["ret_decay_fwd"], w["ret_decay_bwd"], w["ret_gn_w"])
    att = _diff_attention(p3, w["lq1"], w["lk1"], w["lq2"], w["lk2"], w["diff_subln_w"], layer_idx)
    x1, h2 = _out_proj(ret.reshape(nb * seq, -1), att.reshape(nb * seq, -1), x2d, mod,
                       w["w_out"], w["mix_post_w"], w["ffn_pre_w"], seq)
    a, b, c, r = _peer_route(h2, w["wq_t"], w["sub_keys"])
    o_t = _peer_experts(h2, w["u"], w["v_t"], a, b, c, r)
    y = _final(o_t, x1, mod, w["ffn_post_w"], seq)
    return y.reshape(nb, seq, d)


def kernel(x_prompt, x_sample, c_prompt, c_sample, ada_w, ada_b, mix_pre_w, mix_post_w, ffn_pre_w,
           ffn_post_w, w_in, w_out, ret_decay_fwd, ret_decay_bwd, ret_gn_w, diff_lambda_q1,
           diff_lambda_k1, diff_lambda_q2, diff_lambda_k2, diff_subln_w, peer_w_query, peer_sub_keys,
           peer_u, peer_v):
    depth = ada_w.shape[0]
    d = x_prompt.shape[-1]
    n_prompt = x_prompt.shape[0]
    xs = [x_prompt, x_sample]
    c_all = jnp.concatenate([c_prompt, c_sample], axis=0)
    for l in range(depth):
        mod = _adaln(c_all, ada_w[l], ada_b[l]).reshape(c_all.shape[0], 6, d)
        w = {
            "mix_pre_w": mix_pre_w[l].reshape(1, d), "mix_post_w": mix_post_w[l].reshape(1, d),
            "ffn_pre_w": ffn_pre_w[l].reshape(1, d), "ffn_post_w": ffn_post_w[l].reshape(1, d),
            "w_in": w_in[l].astype(BF16), "w_out": w_out[l].astype(BF16),
            "ret_decay_fwd": ret_decay_fwd[l], "ret_decay_bwd": ret_decay_bwd[l],
            "ret_gn_w": ret_gn_w[l],
            "lq1": diff_lambda_q1[l], "lk1": diff_lambda_k1[l],
            "lq2": diff_lambda_q2[l], "lk2": diff_lambda_k2[l],
            "diff_subln_w": diff_subln_w[l],
            "wq_t": peer_w_query[l].T.astype(BF16),
            "sub_keys": peer_sub_keys[l].reshape(2 * PEER_HEADS, PEER_NKEYS, PEER_HALF).astype(BF16),
            "u": peer_u[l].astype(BF16), "v_t": peer_v[l].T.astype(BF16),
        }
        xs = [_layer(xs[0], mod[:n_prompt], l, w, l), _layer(xs[1], mod[n_prompt:], l, w, l)]
    return (xs[0], xs[1])
```

```python
import functools
import math

import jax
import jax.numpy as jnp
from jax import lax
from jax.experimental import pallas as pl
from jax.experimental.pallas import tpu as pltpu

F32 = jnp.float32
BF16 = jnp.bfloat16
EPS = 1e-6

RET_HEADS = 8
RET_D = 128
DIFF_HEADS = 8
DIFF_DQK = 64
DIFF_DV = 128
PEER_HEADS = 8
PEER_NKEYS = 128
PEER_HALF = 128
PEER_TOPK = 16
RET_CHUNK = 256

VMEM_LIMIT = 56 * 1024 * 1024


def _params(sem):
    return pltpu.CompilerParams(dimension_semantics=sem, vmem_limit_bytes=VMEM_LIMIT)


def _rms(x, w):
    return x * lax.rsqrt(jnp.mean(x * x, axis=-1, keepdims=True) + EPS) * w


def _adaln_kernel(c_ref, w_ref, b_ref, o_ref):
    c = c_ref[...]
    a = c / (1.0 + jnp.exp(-c))
    o_ref[...] = jnp.dot(a.astype(BF16), w_ref[...].astype(BF16),
                         preferred_element_type=F32) + b_ref[...]


def _adaln(c, w, b):
    nb, d = c.shape
    n_out = w.shape[1]
    tn = 1024
    return pl.pallas_call(
        _adaln_kernel,
        grid=(n_out // tn,),
        in_specs=[pl.BlockSpec((nb, d), lambda j: (0, 0)),
                  pl.BlockSpec((d, tn), lambda j: (0, j)),
                  pl.BlockSpec((1, tn), lambda j: (0, j))],
        out_specs=pl.BlockSpec((nb, tn), lambda j: (0, j)),
        out_shape=jax.ShapeDtypeStruct((nb, n_out), F32),
        compiler_params=_params(("arbitrary",)),
        name="adaln",
    )(c, w, b.reshape(1, n_out))


def _proj_kernel(x_ref, mod_ref, nw_ref, w_ref, o_ref, h_ref):
    @pl.when(pl.program_id(1) == 0)
    def _():
        h = _rms(x_ref[...], nw_ref[...]) * (1.0 + mod_ref[0, 1:2, :]) + mod_ref[0, 0:1, :]
        h_ref[...] = h.astype(BF16)

    o_ref[...] = jnp.dot(h_ref[...], w_ref[...], preferred_element_type=F32).astype(BF16)


def _in_proj(x2d, mod, norm_w, w_bf, seq):
    n, d = x2d.shape
    p_in = w_bf.shape[1]
    tm = min(1024, seq)
    tn = 1024
    per_batch = seq // tm
    return pl.pallas_call(
        _proj_kernel,
        grid=(n // tm, p_in // tn),
        in_specs=[pl.BlockSpec((tm, d), lambda i, j: (i, 0)),
                  pl.BlockSpec((1, 6, d), lambda i, j: (i // per_batch, 0, 0)),
                  pl.BlockSpec((1, d), lambda i, j: (0, 0)),
                  pl.BlockSpec((d, tn), lambda i, j: (0, j))],
        out_specs=pl.BlockSpec((tm, tn), lambda i, j: (i, j)),
        out_shape=jax.ShapeDtypeStruct((n, p_in), BF16),
        scratch_shapes=[pltpu.VMEM((tm, d), BF16)],
        compiler_params=_params(("arbitrary", "arbitrary")),
        name="in_proj",
    )(x2d, mod, norm_w, w_bf)


def _log_sigmoid(x):
    return jnp.minimum(x, 0.0) - jnp.log1p(jnp.exp(-jnp.abs(x)))


def _retention_kernel(q_ref, k_ref, v_ref, g_ref, df_ref, db_ref, gn_ref, o_ref, kv_ref, s_ref,
                      *, seq):
    c = RET_CHUNK
    nc = seq // c
    d = RET_D
    lgf = _log_sigmoid(df_ref[0])
    lgb = _log_sigmoid(db_ref[0])
    scale = d ** -0.5

    ri = lax.broadcasted_iota(jnp.int32, (c, c), 0)
    ci = lax.broadcasted_iota(jnp.int32, (c, c), 1)
    rel = (ri - ci).astype(F32)
    dmat = jnp.where(rel >= 0.0,
                     jnp.exp(lgf * jnp.maximum(rel, 0.0)),
                     jnp.exp(lgb * jnp.maximum(-rel, 0.0))) * scale
    pos = lax.broadcasted_iota(jnp.int32, (c, d), 0).astype(F32)
    lgf_d = lgf[:, :d]
    lgb_d = lgb[:, :d]
    q_dec_f = jnp.exp(lgf_d * pos)
    q_dec_b = jnp.exp(lgb_d * (c - 1.0 - pos))
    k_dec_f = jnp.exp(lgf_d * (c - pos)) * scale
    k_dec_b = jnp.exp(lgb_d * (pos + 1.0)) * scale
    chunk_f = jnp.exp(lgf_d * float(c))
    chunk_b = jnp.exp(lgb_d * float(c))

    for n in range(nc):
        kc = k_ref[0, n * c:(n + 1) * c, :].astype(F32)
        vc = v_ref[0, n * c:(n + 1) * c, :]
        kk = jnp.concatenate([kc * k_dec_f, kc * k_dec_b], axis=1)
        kv_ref[n] = jnp.dot(kk.T.astype(BF16), vc, preferred_element_type=F32)

    state = jnp.zeros((d, d), F32)
    for n in range(nc):
        s_ref[n, 0:d, :] = state.astype(BF16)
        state = state * chunk_f + kv_ref[n, 0:d, :]
    state = jnp.zeros((d, d), F32)
    for n in range(nc - 1, -1, -1):
        s_ref[n, d:2 * d, :] = state.astype(BF16)
        state = state * chunk_b + kv_ref[n, d:2 * d, :]

    gn_w = gn_ref[...]
    for n in range(nc):
        sl = slice(n * c, (n + 1) * c)
        qc = q_ref[0, sl, :]
        kc = k_ref[0, sl, :]
        vc = v_ref[0, sl, :]
        sc = lax.dot_general(qc, kc, (((1,), (1,)), ((), ())), preferred_element_type=F32)
        intra = jnp.dot((sc * dmat).astype(BF16), vc, preferred_element_type=F32)
        qf = qc.astype(F32)
        qq = jnp.concatenate([qf * q_dec_f, qf * q_dec_b], axis=1).astype(BF16)
        cross = jnp.dot(qq, s_ref[n], preferred_element_type=F32)
        o = intra + cross
        mu = jnp.mean(o, axis=-1, keepdims=True)
        var = jnp.mean(jnp.square(o - mu), axis=-1, keepdims=True)
        o = (o - mu) * lax.rsqrt(var + EPS) * gn_w
        g = g_ref[0, sl, :].astype(F32)
        o_ref[0, sl, :] = ((g / (1.0 + jnp.exp(-g))) * o).astype(BF16)


def _retention(p3, decay_f, decay_b, gn_w):
    nb, seq, _ = p3.shape
    h, d, c = RET_HEADS, RET_D, RET_CHUNK
    df = jnp.broadcast_to(decay_f.astype(F32)[:, None, None], (h, 1, c))
    db = jnp.broadcast_to(decay_b.astype(F32)[:, None, None], (h, 1, c))
    blk = lambda off: pl.BlockSpec((1, seq, d), lambda b, hh: (b, 0, off + hh))
    return pl.pallas_call(
        functools.partial(_retention_kernel, seq=seq),
        grid=(nb, h),
        in_specs=[blk(0), blk(h), blk(2 * h), blk(3 * h),
                  pl.BlockSpec((1, 1, c), lambda b, hh: (hh, 0, 0)),
                  pl.BlockSpec((1, 1, c), lambda b, hh: (hh, 0, 0)),
                  pl.BlockSpec((1, d), lambda b, hh: (0, hh))],
        out_specs=pl.BlockSpec((1, seq, d), lambda b, hh: (b, 0, hh)),
        out_shape=jax.ShapeDtypeStruct((nb, seq, h * d), BF16),
        scratch_shapes=[pltpu.VMEM((seq // c, 2 * d, d), F32),
                        pltpu.VMEM((seq // c, 2 * d, d), BF16)],
        compiler_params=_params(("arbitrary", "arbitrary")),
        name="retention",
    )(p3, p3, p3, p3, df, db, gn_w.reshape(1, h * d))


def _diff_kernel(q_ref, k_ref, v_ref, slope_ref, lq1_ref, lk1_ref, lq2_ref, lk2_ref, sw_ref,
                 o_ref, bias_ref, *, tq, seq, lam_init):
    qi = pl.program_id(1)

    @pl.when(pl.program_id(2) == 0)
    def _():
        qpos = (lax.broadcasted_iota(jnp.int32, (tq, seq), 0) + qi * tq).astype(F32)
        kpos = lax.broadcasted_iota(jnp.int32, (tq, seq), 1).astype(F32)
        bias_ref[...] = slope_ref[0] * jnp.abs(qpos - kpos)

    lam = (jnp.exp(jnp.sum(lq1_ref[...] * lk1_ref[...], axis=-1, keepdims=True))
           - jnp.exp(jnp.sum(lq2_ref[...] * lk2_ref[...], axis=-1, keepdims=True)) + lam_init)

    q = q_ref[0]
    lane = lax.broadcasted_iota(jnp.int32, q.shape, 1)
    zero = jnp.zeros_like(q)
    k = k_ref[0]
    nt = (((1,), (1,)), ((), ()))
    bias = bias_ref[...]
    scale = DIFF_DQK ** -0.5

    def probs(qm):
        s = lax.dot_general(qm, k, nt, preferred_element_type=F32) * scale - bias
        e = jnp.exp(s - jnp.max(s, axis=-1, keepdims=True))
        return e, jnp.sum(e, axis=-1, keepdims=True)

    e1, l1 = probs(jnp.where(lane < DIFF_DQK, q, zero))
    e2, l2 = probs(jnp.where(lane >= DIFF_DQK, q, zero))
    a = e1 * (1.0 / l1) - e2 * (lam / l2)
    od = jnp.dot(a.astype(BF16), v_ref[0], preferred_element_type=F32)
    od = od * lax.rsqrt(jnp.mean(od * od, axis=-1, keepdims=True) + EPS) * sw_ref[...] * (1.0 - lam_init)
    o_ref[0] = od.astype(BF16)


def _diff_attention(p3, lq1, lk1, lq2, lk2, subln_w, layer_idx):
    nb, seq, _ = p3.shape
    h = DIFF_HEADS
    d = DIFF_DV
    tq = min(256, seq)
    lam_init = 0.8 - 0.6 * math.exp(-0.3 * layer_idx)
    slopes = 2.0 ** (-8.0 * jnp.arange(1, h + 1, dtype=F32) / h)
    slopes = jnp.broadcast_to(slopes[:, None, None], (h, 1, seq))
    off_q = 4 * RET_HEADS
    off_k = off_q + h
    off_v = off_k + h
    vec = lambda: pl.BlockSpec((1, DIFF_DQK), lambda hh, qi, b: (0, 0))
    return pl.pallas_call(
        functools.partial(_diff_kernel, tq=tq, seq=seq, lam_init=lam_init),
        grid=(h, seq // tq, nb),
        in_specs=[pl.BlockSpec((1, tq, d), lambda hh, qi, b: (b, qi, off_q + hh)),
                  pl.BlockSpec((1, seq, d), lambda hh, qi, b: (b, 0, off_k + hh)),
                  pl.BlockSpec((1, seq, d), lambda hh, qi, b: (b, 0, off_v + hh)),
                  pl.BlockSpec((1, 1, seq), lambda hh, qi, b: (hh, 0, 0)),
                  vec(), vec(), vec(), vec(),
                  pl.BlockSpec((1, d), lambda hh, qi, b: (0, 0))],
        out_specs=pl.BlockSpec((1, tq, d), lambda hh, qi, b: (b, qi, hh)),
        out_shape=jax.ShapeDtypeStruct((nb, seq, h * d), BF16),
        scratch_shapes=[pltpu.VMEM((tq, seq), F32)],
        compiler_params=_params(("arbitrary", "arbitrary", "arbitrary")),
        name="diff_attention",
    )(p3, p3, p3, slopes, lq1.reshape(1, -1), lk1.reshape(1, -1), lq2.reshape(1, -1),
      lk2.reshape(1, -1), subln_w.reshape(1, d))


def _out_kernel(r_ref, a_ref, x_ref, mod_ref, w_ref, postw_ref, prew_ref, x1_ref, h2_ref):
    half = r_ref.shape[1]
    o = (jnp.dot(r_ref[...], w_ref[0:half, :], preferred_element_type=F32)
         + jnp.dot(a_ref[...], w_ref[half:2 * half, :], preferred_element_type=F32))
    x1 = x_ref[...] + mod_ref[0, 2:3, :] * _rms(o, postw_ref[...])
    x1_ref[...] = x1
    h2 = _rms(x1, prew_ref[...]) * (1.0 + mod_ref[0, 4:5, :]) + mod_ref[0, 3:4, :]
    h2_ref[...] = h2.astype(BF16)


def _out_proj(ret, att, x2d, mod, w_bf, post_w, pre_w, seq):
    n, d = x2d.shape
    half = ret.shape[1]
    tm = min(512, seq)
    per_batch = seq // tm
    row = lambda: pl.BlockSpec((1, d), lambda i: (0, 0))
    return pl.pallas_call(
        _out_kernel,
        grid=(n // tm,),
        in_specs=[pl.BlockSpec((tm, half), lambda i: (i, 0)),
                  pl.BlockSpec((tm, half), lambda i: (i, 0)),
                  pl.BlockSpec((tm, d), lambda i: (i, 0)),
                  pl.BlockSpec((1, 6, d), lambda i: (i // per_batch, 0, 0)),
                  pl.BlockSpec((2 * half, d), lambda i: (0, 0)),
                  row(), row()],
        out_specs=[pl.BlockSpec((tm, d), lambda i: (i, 0)),
                   pl.BlockSpec((tm, d), lambda i: (i, 0))],
        out_shape=[jax.ShapeDtypeStruct((n, d), F32), jax.ShapeDtypeStruct((n, d), BF16)],
        compiler_params=_params(("arbitrary",)),
        name="out_proj",
    )(ret, att, x2d, mod, w_bf, post_w, pre_w)


def _hyperbola_cells():
    k = PEER_TOPK
    return [(a, b) for a in range(k) for b in range(k) if (a + 1) * (b + 1) <= k]


def _route_kernel(h_ref, wq_ref, sk_ref, a_ref, b_ref, c_ref, r_ref,
                  q_s, s_s, e1_s, r1_s, sv_s):
    nk, topk, nh = PEER_NKEYS, PEER_TOPK, PEER_HEADS
    lw = 128
    tk = h_ref.shape[0]
    nt = (((1,), (1,)), ((), ()))
    q_s[...] = lax.dot_general(wq_ref[...], h_ref[...], nt, preferred_element_type=F32).astype(BF16)
    key_idx = lax.broadcasted_iota(jnp.int32, (nk, lw), 0).astype(F32)

    def top16(s, break_ties):
        work = s
        rank = jnp.full((nk, lw), float(topk), F32)
        vals = []
        for r in range(topk):
            m = jnp.max(work, axis=0, keepdims=True)
            sel = work == m
            if break_ties:
                first = jnp.min(jnp.where(sel, key_idx, float(nk)), axis=0, keepdims=True)
                sel = key_idx == first
            rank = jnp.where(sel, float(r), rank)
            work = jnp.where(sel, -jnp.inf, work)
            vals.append(m)
        return rank, vals

    for hp in range(2 * nh):
        s_s[hp % 2, hp // 2] = jnp.dot(sk_ref[hp], q_s[hp * PEER_HALF:(hp + 1) * PEER_HALF, :],
                                       preferred_element_type=F32)

    cells = _hyperbola_cells()
    head_idx = lax.broadcasted_iota(jnp.int32, (nh, lw), 0)

    def staircase(sv1, sv2):
        vals = [sv1[a] + sv2[b] for a, b in cells]
        beaten = [jnp.full((nh, lw), float((a + 1) * (b + 1) - 1), F32) for a, b in cells]
        for x, (ax, bx) in enumerate(cells):
            for y in range(x + 1, len(cells)):
                ay, by = cells[y]
                if ax <= ay and bx <= by:
                    continue
                x_wins = (vals[x] >= vals[y]).astype(F32)
                beaten[y] = beaten[y] + x_wins
                beaten[x] = beaten[x] + (1.0 - x_wins)
        z = jnp.zeros((nh, lw), F32)
        count = [jnp.zeros((nh, lw), F32) for _ in range(topk)]
        for i, (a, b) in enumerate(cells):
            sel = (beaten[i] < float(topk)).astype(F32)
            count[a] = count[a] + sel
            z = z + sel * jnp.exp(vals[i] - vals[0])
        return count, 1.0 / z

    def per_group(g, carry):
        lanes = pl.ds(pl.multiple_of(g * lw, lw), lw)

        def per_head(h, sv):
            sv = list(sv)
            s = [s_s[p, h, :, lanes] for p in range(2)]

            def emit(p, rank, vals):
                e = jnp.exp(s[p] - vals[0])
                sv_s[p] = jnp.concatenate(vals, axis=0)
                if p == 0:
                    e1_s[h, :, lanes] = e
                    r1_s[h, :, lanes] = rank
                else:
                    b_ref[h, :, lanes] = e.astype(BF16)
                    r_ref[h, :, lanes] = rank.astype(BF16)

            tied = 0.0
            for p, (rank, vals) in enumerate([top16(s[0], False), top16(s[1], False)]):
                emit(p, rank, vals)
                picked = jnp.sum((rank < float(topk)).astype(F32), axis=0, keepdims=True)
                tied = tied + jnp.max(jnp.abs(picked - float(topk)))

            @pl.when(tied > 0.0)
            def _():
                for p in range(2):
                    emit(p, *top16(s[p], True))

            for p in range(2):
                sorted_vals = sv_s[p]
                for r in range(topk):
                    sv[p * topk + r] = jnp.where(head_idx == h, sorted_vals[r:r + 1, :], sv[p * topk + r])
            return tuple(sv)

        sv = lax.fori_loop(0, nh, per_head, tuple(jnp.zeros((nh, lw), F32) for _ in range(2 * topk)))
        count, inv_z = staircase(sv[:topk], sv[topk:])
        for h in range(nh):
            rank1 = r1_s[h, :, lanes]
            cvec = jnp.zeros((nk, lw), F32)
            for a in range(topk):
                cvec = jnp.where(rank1 == float(a), count[a][h:h + 1, :], cvec)
            c_ref[h, :, lanes] = cvec
            a_ref[h, :, lanes] = e1_s[h, :, lanes] * inv_z[h:h + 1, :]
        return carry

    lax.fori_loop(0, tk // lw, per_group, 0)


def _peer_route(h2, wq_t, sub_keys):
    n, d = h2.shape
    nh, nk, topk = PEER_HEADS, PEER_NKEYS, PEER_TOPK
    tk = 512
    out32 = jax.ShapeDtypeStruct((nh, nk, n), F32)
    out16 = jax.ShapeDtypeStruct((nh, nk, n), BF16)
    ospec = lambda: pl.BlockSpec((nh, nk, tk), lambda i: (0, 0, i))
    return pl.pallas_call(
        _route_kernel,
        grid=(n // tk,),
        in_specs=[pl.BlockSpec((tk, d), lambda i: (i, 0)),
                  pl.BlockSpec(wq_t.shape, lambda i: (0, 0)),
                  pl.BlockSpec(sub_keys.shape, lambda i: (0, 0, 0))],
        out_specs=[ospec(), ospec(), ospec(), ospec()],
        out_shape=[out32, out16, out32, out16],
        scratch_shapes=[pltpu.VMEM((wq_t.shape[0], tk), BF16),
                        pltpu.VMEM((2, nh, nk, tk), F32),
                        pltpu.VMEM((nh, nk, tk), F32),
                        pltpu.VMEM((nh, nk, tk), F32),
                        pltpu.VMEM((2, topk, 128), F32)],
        compiler_params=_params(("arbitrary",)),
        name="peer_route",
    )(h2, wq_t, sub_keys)


def _expert_kernel(h_ref, ua_ref, ub_ref, va_ref, vb_ref, a_ref, b_ref, c_ref, r_ref, o_ref, wb_s,
                   *, rows, nsteps):
    j = pl.program_id(1)
    nk = PEER_NKEYS
    nt = (((1,), (1,)), ((), ()))

    def gated(hid, first_row):
        out = []
        for ii in range(rows):
            i = first_row + ii
            x = hid[ii * nk:(ii + 1) * nk, :]
            gate = jnp.zeros(x.shape, BF16)
            for hd in range(PEER_HEADS):
                a = a_ref[hd, pl.ds(i, 1), :].astype(BF16)
                c = c_ref[hd, pl.ds(i, 1), :].astype(BF16)
                gate = gate + jnp.where(r_ref[hd] < c, b_ref[hd], jnp.zeros((), BF16)) * a
            act = 0.5 * x * (1.0 + lax.erf(x * (2.0 ** -0.5)))
            out.append((gate.astype(F32) * act).astype(BF16))
        return jnp.concatenate(out, axis=0)

    @pl.when(j == 0)
    def _():
        wb_s[...] = jnp.zeros_like(wb_s)
        o_ref[...] = jnp.zeros_like(o_ref)

    @pl.when(j < nsteps)
    def _():
        h = h_ref[...]
        hid_a = lax.dot_general(ua_ref[...], h, nt, preferred_element_type=F32)
        part_b = jnp.dot(vb_ref[...], wb_s[...], preferred_element_type=F32)
        w_a = gated(hid_a, 2 * j * rows)
        hid_b = lax.dot_general(ub_ref[...], h, nt, preferred_element_type=F32)
        part_a = jnp.dot(va_ref[...], w_a, preferred_element_type=F32)
        wb_s[...] = gated(hid_b, (2 * j + 1) * rows)
        o_ref[...] += part_b + part_a

    @pl.when(j == nsteps)
    def _():
        o_ref[...] += jnp.dot(vb_ref[...], wb_s[...], preferred_element_type=F32)


def _peer_experts(h2, u_bf, vt_bf, a, b, c, r):
    n, d = h2.shape
    n_exp = u_bf.shape[0]
    nh, nk = PEER_HEADS, PEER_NKEYS
    tn = 512
    rows = 4
    eb = rows * nk
    nsteps = n_exp // (2 * eb)
    last = 2 * nsteps - 1
    gspec = lambda: pl.BlockSpec((nh, nk, tn), lambda i, j: (0, 0, i))
    return pl.pallas_call(
        functools.partial(_expert_kernel, rows=rows, nsteps=nsteps),
        grid=(n // tn, nsteps + 1),
        in_specs=[pl.BlockSpec((tn, d), lambda i, j: (i, 0)),
                  pl.BlockSpec((eb, d), lambda i, j: (jnp.minimum(2 * j, last - 1), 0)),
                  pl.BlockSpec((eb, d), lambda i, j: (jnp.minimum(2 * j + 1, last), 0)),
                  pl.BlockSpec((d, eb), lambda i, j: (0, jnp.minimum(2 * j, last - 1))),
                  pl.BlockSpec((d, eb), lambda i, j: (0, jnp.maximum(2 * j - 1, 0))),
                  gspec(), gspec(), gspec(), gspec()],
        out_specs=pl.BlockSpec((d, tn), lambda i, j: (0, i)),
        out_shape=jax.ShapeDtypeStruct((d, n), F32),
        scratch_shapes=[pltpu.VMEM((eb, tn), BF16)],
        compiler_params=_params(("arbitrary", "arbitrary")),
        name="peer_experts",
    )(h2, u_bf, u_bf, vt_bf, vt_bf, a, b, c, r)


def _final_kernel(ot_ref, x1_ref, mod_ref, w_ref, y_ref):
    o = ot_ref[...].T
    y_ref[...] = x1_ref[...] + mod_ref[0, 5:6, :] * _rms(o, w_ref[...])


def _final(o_t, x1, mod, post_w, seq):
    n, d = x1.shape
    tm = min(512, seq)
    per_batch = seq // tm
    return pl.pallas_call(
        _final_kernel,
        grid=(n // tm,),
        in_specs=[pl.BlockSpec((d, tm), lambda i: (0, i)),
                  pl.BlockSpec((tm, d), lambda i: (i, 0)),
                  pl.BlockSpec((1, 6, d), lambda i: (i // per_batch, 0, 0)),
                  pl.BlockSpec((1, d), lambda i: (0, 0))],
        out_specs=pl.BlockSpec((tm, d), lambda i: (i, 0)),
        out_shape=jax.ShapeDtypeStruct((n, d), F32),
        compiler_params=_params(("arbitrary",)),
        name="final_residual",
    )(o_t, x1, mod, post_w)


def _layer(x, mod, l, w, layer_idx):
    nb, seq, d = x.shape
    x2d = x.reshape(nb * seq, d)
    p = _in_proj(x2d, mod, w["mix_pre_w"], w["w_in"], seq)
    p3 = p.reshape(nb, seq, -1)
    ret = _retention(p3, w["ret_decay_fwd"], w["ret_decay_bwd"], w["ret_gn_w"])
    att = _diff_attention(p3, w["lq1"], w["lk1"], w["lq2"], w["lk2"], w["diff_subln_w"], layer_idx)
    x1, h2 = _out_proj(ret.reshape(nb * seq, -1), att.reshape(nb * seq, -1), x2d, mod,
                       w["w_out"], w["mix_post_w"], w["ffn_pre_w"], seq)
    a, b, c, r = _peer_route(h2, w["wq_t"], w["sub_keys"])
    o_t = _peer_experts(h2, w["u"], w["v_t"], a, b, c, r)
    y = _final(o_t, x1, mod, w["ffn_post_w"], seq)
    return y.reshape(nb, seq, d)


def kernel(x_prompt, x_sample, c_prompt, c_sample, ada_w, ada_b, mix_pre_w, mix_post_w, ffn_pre_w,
           ffn_post_w, w_in, w_out, ret_decay_fwd, ret_decay_bwd, ret_gn_w, diff_lambda_q1,
           diff_lambda_k1, diff_lambda_q2, diff_lambda_k2, diff_subln_w, peer_w_query, peer_sub_keys,
           peer_u, peer_v):
    depth = ada_w.shape[0]
    d = x_prompt.shape[-1]
    n_prompt = x_prompt.shape[0]
    xs = [x_prompt, x_sample]
    c_all = jnp.concatenate([c_prompt, c_sample], axis=0)
    for l in range(depth):
        mod = _adaln(c_all, ada_w[l], ada_b[l]).reshape(c_all.shape[0], 6, d)
        w = {
            "mix_pre_w": mix_pre_w[l].reshape(1, d), "mix_post_w": mix_post_w[l].reshape(1, d),
            "ffn_pre_w": ffn_pre_w[l].reshape(1, d), "ffn_post_w": ffn_post_w[l].reshape(1, d),
            "w_in": w_in[l].astype(BF16), "w_out": w_out[l].astype(BF16),
            "ret_decay_fwd": ret_decay_fwd[l], "ret_decay_bwd": ret_decay_bwd[l],
            "ret_gn_w": ret_gn_w[l],
            "lq1": diff_lambda_q1[l], "lk1": diff_lambda_k1[l],
            "lq2": diff_lambda_q2[l], "lk2": diff_lambda_k2[l],
            "diff_subln_w": diff_subln_w[l],
            "wq_t": peer_w_query[l].T.astype(BF16),
            "sub_keys": peer_sub_keys[l].reshape(2 * PEER_HEADS, PEER_NKEYS, PEER_HALF).astype(BF16),
            "u": peer_u[l].astype(BF16), "v_t": peer_v[l].T.astype(BF16),
        }
        xs = [_layer(xs[0], mod[:n_prompt], l, w, l), _layer(xs[1], mod[n_prompt:], l, w, l)]
    return (xs[0], xs[1])
```
